```python
import math
import jax, jax.numpy as jnp
from jax import lax
import numpy as np

D_MODEL = 2048
BATCH = 1
SEQ = 8192
DEPTH = 2

EXPAND = 2
W_MIX = EXPAND * D_MODEL
W_CONV = W_MIX // 2
CONV_GROUPS = 16
CONV_WIDTH = 3
N_RET_HEADS = 8
W_RET = W_MIX - W_CONV
RET_DV = W_RET // N_RET_HEADS
RET_DK = RET_DV // 2
W_QK = N_RET_HEADS * RET_DK
RET_CHUNK = 128
ROPE_THETA = 10000.0
NORM_EPS = 1e-6
W_IN_SPLITS = (W_CONV, W_CONV, W_CONV, W_CONV, W_QK, W_QK, W_RET, W_RET)
W_IN_TOTAL = sum(W_IN_SPLITS)

kernel_name = "hybrid_shortconv_retention_parallel_heads"


def _rmsnorm(x, gain):
    xf = x.astype(jnp.float32)
    y = xf * lax.rsqrt(jnp.mean(xf * xf, axis=-1, keepdims=True) + NORM_EPS)
    return (y * gain.astype(jnp.float32)).astype(x.dtype)


def _rope(t, positions):
    half = t.shape[-1] // 2
    inv_freq = ROPE_THETA ** (-jnp.arange(half, dtype=jnp.float32) / half)
    ang = positions.astype(jnp.float32)[..., None] * inv_freq
    cos, sin = jnp.cos(ang)[:, :, None, :], jnp.sin(ang)[:, :, None, :]
    tf = t.astype(jnp.float32)
    t1, t2 = tf[..., :half], tf[..., half:]
    return jnp.concatenate([t1 * cos - t2 * sin, t1 * sin + t2 * cos], axis=-1)


def _causal_short_conv(u, w):
    s = u.shape[1]
    up = jnp.pad(u, ((0, 0), (CONV_WIDTH - 1, 0), (0, 0)))
    return sum(w[k] * up[:, k:k + s] for k in range(CONV_WIDTH))


def _retention_chunkwise(q, k, v):
    b, s, h, dk = q.shape
    dv = v.shape[-1]
    n = s // RET_CHUNK
    log_gamma = jnp.log(1.0 - 2.0 ** (-5.0 - jnp.arange(h, dtype=jnp.float32)))
    idx = jnp.arange(RET_CHUNK, dtype=jnp.float32)
    diff = idx[:, None] - idx[None, :]
    decay_in = jnp.where(diff[None] >= 0,
                         jnp.exp(jnp.maximum(diff, 0.0)[None] * log_gamma[:, None, None]), 0.0)
    zeta = jnp.exp((RET_CHUNK - 1 - idx)[:, None] * log_gamma[None])
    xi = jnp.exp((idx + 1.0)[:, None] * log_gamma[None])
    gamma_chunk = jnp.exp(RET_CHUNK * log_gamma)

    qc = q.reshape(b, n, RET_CHUNK, h, dk)
    kc = k.reshape(b, n, RET_CHUNK, h, dk)
    vc = v.reshape(b, n, RET_CHUNK, h, dv)

    scores = jnp.einsum('bnihd,bnjhd->bnhij', qc, kc) * decay_in[None, None]
    inner = jnp.einsum('bnhij,bnjhe->bnihe', scores, vc)

    kv = jnp.einsum('bnjhd,bnjhe->bnhde', kc * zeta[None, None, :, :, None], vc)

    def step(state, kv_i):
        return gamma_chunk[None, :, None, None] * state + kv_i, state

    _, states_prev = lax.scan(step, jnp.zeros((b, h, dk, dv), jnp.float32), jnp.moveaxis(kv, 1, 0))
    states_prev = jnp.moveaxis(states_prev, 0, 1)
    cross = jnp.einsum('bnihd,bnhde->bnihe', qc, states_prev) * xi[None, None, :, :, None]
    return (inner + cross).reshape(b, s, h, dv)


def _head_groupnorm(y, gain):
    mu = jnp.mean(y, axis=-1, keepdims=True)
    var = jnp.mean(jnp.square(y - mu), axis=-1, keepdims=True)
    yn = (y - mu) * lax.rsqrt(var + NORM_EPS)
    return yn.reshape(y.shape[0], y.shape[1], -1) * gain.astype(jnp.float32)


def _mixer_layer(x, c, positions, norm_pre, norm_post, w_ada, b_ada, w_in, conv_w, gn_gain, w_out):
    b, s, _ = x.shape
    mod = jax.nn.silu(c) @ w_ada + b_ada
    shift, scale, gate = jnp.split(mod, 3, axis=-1)
    h = _rmsnorm(x, norm_pre) * (1.0 + scale[:, None, :]) + shift[:, None, :]

    z = h @ w_in
    offs = list(np.cumsum(W_IN_SPLITS)[:-1])
    gb, gcin, u, g_c, q, k, v, g_r = jnp.split(z, offs, axis=-1)

    y_conv = gb * _causal_short_conv(gcin * u, conv_w)
    y_conv = y_conv * jax.nn.silu(g_c)

    q = _rope(q.reshape(b, s, N_RET_HEADS, RET_DK), positions)
    k = _rope(k.reshape(b, s, N_RET_HEADS, RET_DK), positions) * (RET_DK ** -0.5)
    v = v.reshape(b, s, N_RET_HEADS, RET_DV).astype(jnp.float32)
    ret = _head_groupnorm(_retention_chunkwise(q, k, v), gn_gain).astype(x.dtype)
    y_ret = ret * jax.nn.silu(g_r)

    o = jnp.concatenate([y_conv, y_ret], axis=-1) @ w_out
    return x + gate[:, None, :] * _rmsnorm(o, norm_post)


def setup_inputs(seed: int = 0) -> dict:
    key = jax.random.key(seed)
    ks = jax.random.split(key, 12)
    f32 = jnp.float32
    x = jax.random.normal(ks[0], (BATCH, SEQ, D_MODEL), f32)
    c = jax.random.normal(ks[1], (BATCH, D_MODEL), f32)
    offset = jax.random.randint(ks[2], (BATCH, 1), 0, 1024, dtype=jnp.int32)
    positions = jnp.arange(SEQ, dtype=jnp.int32)[None, :] + offset
    norm_pre = 1.0 + 0.05 * jax.random.normal(ks[3], (DEPTH, D_MODEL), f32)
    norm_post = 1.0 + 0.05 * jax.random.normal(ks[4], (DEPTH, D_MODEL), f32)
    w_ada = 0.1 * D_MODEL ** -0.5 * jax.random.normal(ks[5], (DEPTH, D_MODEL, 3 * D_MODEL), f32)
    b_ada = 0.01 * jax.random.normal(ks[6], (DEPTH, 3 * D_MODEL), f32)
    w_in = D_MODEL ** -0.5 * jax.random.normal(ks[7], (DEPTH, D_MODEL, W_IN_TOTAL), f32)
    conv_w = CONV_WIDTH ** -0.5 * jax.random.normal(ks[8], (DEPTH, CONV_WIDTH, W_CONV), f32)
    gn_gain = 1.0 + 0.05 * jax.random.normal(ks[9], (DEPTH, W_RET), f32)
    w_out = W_MIX ** -0.5 * jax.random.normal(ks[10], (DEPTH, W_MIX, D_MODEL), f32)
    return {"x": x, "c": c, "positions": positions, "norm_pre": norm_pre, "norm_post": norm_post,
            "w_ada": w_ada, "b_ada": b_ada, "w_in": w_in, "conv_w": conv_w,
            "gn_gain": gn_gain, "w_out": w_out}


def reference(x, c, positions, norm_pre, norm_post, w_ada, b_ada, w_in, conv_w, gn_gain, w_out):
    for l in range(DEPTH):
        x = _mixer_layer(x, c, positions, norm_pre[l], norm_post[l], w_ada[l], b_ada[l],
                         w_in[l], conv_w[l], gn_gain[l], w_out[l])
    return x
```

```python
import functools

import jax
import jax.numpy as jnp
from jax import lax
from jax.experimental import pallas as pl
from jax.experimental.pallas import tpu as pltpu

CONV_WIDTH = 3
N_RET_HEADS = 8
RET_CHUNK = 128
ROPE_THETA = 10000.0
NORM_EPS = 1e-6

V7X_VMEM_BYTES = 64 * 1024 * 1024
SUBLANES = 8
LANES = 128

F32 = jnp.float32
BF16 = jnp.bfloat16


def _silu(x):
    return x * (1.0 / (1.0 + jnp.exp(-x)))


def _rms_scale(x, d):
    return lax.rsqrt(jnp.sum(x * x, axis=-1, keepdims=True) * (1.0 / d) + NORM_EPS)


def _ada_kernel(c_ref, w_ref, b_ref, o_ref):
    s = _silu(c_ref[...])
    acc = jnp.sum(w_ref[0] * s, axis=0, keepdims=True)
    o_ref[0] = acc + b_ref[0]


def _ada_call(c_col, w_ada, b_ada3, tn):
    depth, d, n = w_ada.shape
    return pl.pallas_call(
        _ada_kernel,
        grid=(depth, n // tn),
        in_specs=[
            pl.BlockSpec((d, 1), lambda l, t: (0, 0)),
            pl.BlockSpec((1, d, tn), lambda l, t: (l, 0, t)),
            pl.BlockSpec((1, 1, tn), lambda l, t: (l, 0, t)),
        ],
        out_specs=pl.BlockSpec((1, 1, tn), lambda l, t: (l, 0, t)),
        out_shape=jax.ShapeDtypeStruct((depth, 1, n), F32),
        compiler_params=pltpu.CompilerParams(
            dimension_semantics=("arbitrary", "arbitrary"),
            vmem_limit_bytes=40 * 1024 * 1024),
        name="ada_mod",
    )(c_col, w_ada, b_ada3)


def _rope_kernel(pos_ref, freq_ref, sign_ref, cos_ref, sin_ref):
    ang = pos_ref[...].astype(F32) * freq_ref[...]
    cos_ref[...] = jnp.cos(ang)
    sin_ref[...] = jnp.sin(ang) * sign_ref[...]


def _rope_call(pos_col, freq_full, sign_full, ts):
    s = pos_col.shape[0]
    dk = freq_full.shape[1]
    return pl.pallas_call(
        _rope_kernel,
        grid=(s // ts,),
        in_specs=[
            pl.BlockSpec((ts, 1), lambda i: (i, 0)),
            pl.BlockSpec((1, dk), lambda i: (0, 0)),
            pl.BlockSpec((1, dk), lambda i: (0, 0)),
        ],
        out_specs=[pl.BlockSpec((ts, dk), lambda i: (i, 0)),
                   pl.BlockSpec((ts, dk), lambda i: (i, 0))],
        out_shape=[jax.ShapeDtypeStruct((s, dk), F32),
                   jax.ShapeDtypeStruct((s, dk), F32)],
        compiler_params=pltpu.CompilerParams(dimension_semantics=("arbitrary",)),
        name="rope_tables",
    )(pos_col, freq_full, sign_full)


def _prenorm_kernel(x_ref, gain_ref, scale_ref, shift_ref, h_ref):
    x = x_ref[...]
    d = x.shape[-1]
    y = x * _rms_scale(x, d) * gain_ref[...]
    h_ref[...] = (y * (1.0 + scale_ref[...]) + shift_ref[...]).astype(h_ref.dtype)


def _prenorm_call(x, gain, scale, shift, tm):
    s, d = x.shape
    vec = pl.BlockSpec((1, d), lambda i: (0, 0))
    return pl.pallas_call(
        _prenorm_kernel,
        grid=(s // tm,),
        in_specs=[pl.BlockSpec((tm, d), lambda i: (i, 0)), vec, vec, vec],
        out_specs=pl.BlockSpec((tm, d), lambda i: (i, 0)),
        out_shape=jax.ShapeDtypeStruct((s, d), BF16),
        compiler_params=pltpu.CompilerParams(dimension_semantics=("arbitrary",)),
        name="prenorm",
    )(x, gain, scale, shift)


def _mixer_kernel(h_ref, wb_ref, wc_ref, wu_ref, wg_ref, wq_ref, wk_ref, wv_ref, wr_ref,
                  convw_ref, gain_ref, lg_ref, cos_ref, sin_ref, woc_ref, wor_ref,
                  yc_ref, yr_ref, woc_out, wor_out,
                  w_sc, state_sc, carry_sc, *, gw, dk, dv, tm):
    i = pl.program_id(1)
    o_b, o_c, o_u, o_g = 0, gw, 2 * gw, 3 * gw
    o_q = 4 * gw
    o_k = o_q + dk
    o_v = o_k + dk
    o_r = o_v + dv

    @pl.when(i == 0)
    def _():
        w_sc[:, o_b:o_b + gw] = wb_ref[...].astype(BF16)
        w_sc[:, o_c:o_c + gw] = wc_ref[...].astype(BF16)
        w_sc[:, o_u:o_u + gw] = wu_ref[...].astype(BF16)
        w_sc[:, o_g:o_g + gw] = wg_ref[...].astype(BF16)
        w_sc[:, o_q:o_q + dk] = wq_ref[...].astype(BF16)
        w_sc[:, o_k:o_k + dk] = wk_ref[...].astype(BF16)
        w_sc[:, o_v:o_v + dv] = wv_ref[...].astype(BF16)
        w_sc[:, o_r:o_r + dv] = wr_ref[...].astype(BF16)
        woc_out[...] = woc_ref[...].astype(BF16)
        wor_out[...] = wor_ref[...].astype(BF16)
        state_sc[...] = jnp.zeros_like(state_sc)
        carry_sc[...] = jnp.zeros_like(carry_sc)

    h = h_ref[...]

    def proj(off, width):
        return jnp.dot(h, w_sc[:, off:off + width], preferred_element_type=F32)

    cu = proj(o_c, gw) * proj(o_u, gw)
    zb = proj(o_b, gw)
    gate_c = _silu(proj(o_g, gw))
    w0 = convw_ref[0:1, :]
    w1 = convw_ref[1:2, :]
    w2 = convw_ref[2:3, :]
    conv = w0 * pltpu.roll(cu, 2, 0) + w1 * pltpu.roll(cu, 1, 0) + w2 * cu
    yc_ref[...] = (zb * conv * gate_c).astype(yc_ref.dtype)
    head = cu[0:SUBLANES]
    prev = carry_sc[...]
    row = lax.broadcasted_iota(jnp.int32, head.shape, 0)
    m1 = jnp.where(row >= 1, pltpu.roll(head, 1, 0), pltpu.roll(prev, 1, 0))
    m2 = jnp.where(row >= 2, pltpu.roll(head, 2, 0), pltpu.roll(prev, 2, 0))
    conv_head = w0 * m2 + w1 * m1 + w2 * head
    yc_ref[0:SUBLANES, :] = (zb[0:SUBLANES] * conv_head * gate_c[0:SUBLANES]).astype(yc_ref.dtype)
    carry_sc[...] = cu[tm - SUBLANES:tm]

    cosv = cos_ref[...]
    sinv = sin_ref[...]
    q = proj(o_q, dk)
    k = proj(o_k, dk)
    qr = q * cosv + pltpu.roll(q, dk // 2, 1) * sinv
    kr = (k * cosv + pltpu.roll(k, dk // 2, 1) * sinv) * (dk ** -0.5)
    v = proj(o_v, dv)
    gate_r = _silu(proj(o_r, dv))

    lg = lg_ref[0][0:1, :]
    ch = RET_CHUNK
    ri = lax.broadcasted_iota(jnp.int32, (ch, ch), 0)
    ci = lax.broadcasted_iota(jnp.int32, (ch, ch), 1)
    diff = (ri - ci).astype(F32)
    decay = jnp.where(diff >= 0.0, jnp.exp(jnp.maximum(diff, 0.0) * lg), 0.0)
    rif = ri.astype(F32)
    zeta = jnp.exp((ch - 1.0 - rif) * lg)
    xi = jnp.exp((rif + 1.0) * lg)
    xi_v = jnp.concatenate([xi] * (dv // ch), axis=1) if dv != ch else xi
    gamma_chunk = jnp.exp(ch * lg)
    gamma_v = jnp.concatenate([gamma_chunk] * (dv // ch), axis=1) if dv != ch else gamma_chunk
    gain = gain_ref[...]

    state = state_sc[...]
    for c in range(tm // ch):
        r0 = c * ch
        qc = qr[r0:r0 + ch].astype(BF16)
        kc = kr[r0:r0 + ch]
        vc = v[r0:r0 + ch].astype(BF16)
        scores = lax.dot_general(qc, kc.astype(BF16), (((1,), (1,)), ((), ())),
                                 preferred_element_type=F32) * decay
        inner = jnp.dot(scores.astype(BF16), vc, preferred_element_type=F32)
        cross = jnp.dot(qc, state.astype(BF16), preferred_element_type=F32) * xi_v
        kz = (kc * zeta).astype(BF16)
        kv = lax.dot_general(kz, vc, (((0,), (0,)), ((), ())), preferred_element_type=F32)
        state = gamma_v * state + kv
        y = inner + cross
        mu = jnp.sum(y, axis=-1, keepdims=True) * (1.0 / dv)
        yc = y - mu
        var = jnp.sum(yc * yc, axis=-1, keepdims=True) * (1.0 / dv)
        yn = yc * lax.rsqrt(var + NORM_EPS) * gain
        yr_ref[r0:r0 + ch, :] = (yn * gate_r[r0:r0 + ch]).astype(yr_ref.dtype)
    state_sc[...] = state


def _mixer_call(h, w_in, layer, conv_w, gn_gain2, lg_tab, cos_t, sin_t, w_out, tm):
    s, d = h.shape
    w_conv = conv_w.shape[1]
    n_heads = N_RET_HEADS
    gw = w_conv // n_heads
    w_ret = gn_gain2.shape[1]
    dv = w_ret // n_heads
    dk = dv // 2
    w_qk = n_heads * dk
    b_c, b_u, b_g = w_conv // gw, 2 * w_conv // gw, 3 * w_conv // gw
    b_q = 4 * w_conv // dk
    b_k = b_q + w_qk // dk
    b_v = (4 * w_conv + 2 * w_qk) // dv
    b_r = b_v + w_ret // dv
    n_cols = 4 * gw + 2 * dk + 2 * dv

    def wspec(width, base):
        return pl.BlockSpec((None, d, width), lambda j, i: (layer, 0, base + j))

    kern = functools.partial(_mixer_kernel, gw=gw, dk=dk, dv=dv, tm=tm)
    y_spec_c = pl.BlockSpec((tm, gw), lambda j, i: (i, j))
    y_spec_r = pl.BlockSpec((tm, dv), lambda j, i: (i, j))
    return pl.pallas_call(
        kern,
        grid=(n_heads, s // tm),
        in_specs=[
            pl.BlockSpec((tm, d), lambda j, i: (i, 0)),
            wspec(gw, 0), wspec(gw, b_c), wspec(gw, b_u), wspec(gw, b_g),
            wspec(dk, b_q), wspec(dk, b_k), wspec(dv, b_v), wspec(dv, b_r),
            pl.BlockSpec((CONV_WIDTH, gw), lambda j, i: (0, j)),
            pl.BlockSpec((1, dv), lambda j, i: (0, j)),
            pl.BlockSpec((1, SUBLANES, RET_CHUNK), lambda j, i: (j, 0, 0)),
            pl.BlockSpec((tm, dk), lambda j, i: (i, 0)),
            pl.BlockSpec((tm, dk), lambda j, i: (i, 0)),
            pl.BlockSpec((None, gw, d), lambda j, i: (layer, j, 0)),
            pl.BlockSpec((None, dv, d), lambda j, i: (layer, w_conv // dv + j, 0)),
        ],
        out_specs=[y_spec_c, y_spec_r,
                   pl.BlockSpec((gw, d), lambda j, i: (j, 0)),
                   pl.BlockSpec((dv, d), lambda j, i: (j, 0))],
        out_shape=[jax.ShapeDtypeStruct((s, w_conv), BF16),
                   jax.ShapeDtypeStruct((s, w_ret), BF16),
                   jax.ShapeDtypeStruct((w_conv, d), BF16),
                   jax.ShapeDtypeStruct((w_ret, d), BF16)],
        scratch_shapes=[pltpu.VMEM((d, n_cols), BF16),
                        pltpu.VMEM((dk, dv), F32),
                        pltpu.VMEM((SUBLANES, gw), F32)],
        compiler_params=pltpu.CompilerParams(
            dimension_semantics=("arbitrary", "arbitrary"),
            vmem_limit_bytes=56 * 1024 * 1024),
        name="mixer",
    )(h, w_in, w_in, w_in, w_in, w_in, w_in, w_in, w_in,
      conv_w, gn_gain2, lg_tab, cos_t, sin_t, w_out, w_out)


def _outproj_kernel(yc_ref, yr_ref, wc_ref, wr_ref, x_ref, gate_ref, post_ref,
                    pre_ref, scale_ref, shift_ref, xo_ref, *maybe_h_ref):
    o = jnp.dot(yc_ref[...], wc_ref[...], preferred_element_type=F32)
    o = o + jnp.dot(yr_ref[...], wr_ref[...], preferred_element_type=F32)
    d = o.shape[-1]
    r = o * _rms_scale(o, d) * post_ref[...]
    xn = x_ref[...] + gate_ref[...] * r
    xo_ref[...] = xn
    if maybe_h_ref:
        h_ref, = maybe_h_ref
        y = xn * _rms_scale(xn, d) * pre_ref[...]
        h_ref[...] = (y * (1.0 + scale_ref[...]) + shift_ref[...]).astype(h_ref.dtype)


def _outproj_call(yc, yr, wc, wr, x, gate, post, pre, scale, shift, tm, emit_h):
    s, d = x.shape
    wcw = yc.shape[1]
    wrw = yr.shape[1]
    vec = pl.BlockSpec((1, d), lambda i: (0, 0))
    row = lambda w: pl.BlockSpec((tm, w), lambda i: (i, 0))
    resident = lambda w: pl.BlockSpec((w, d), lambda i: (0, 0), pipeline_mode=pl.Buffered(1))
    out_specs = [row(d)]
    out_shape = [jax.ShapeDtypeStruct((s, d), F32)]
    if emit_h:
        out_specs.append(row(d))
        out_shape.append(jax.ShapeDtypeStruct((s, d), BF16))
    res = pl.pallas_call(
        _outproj_kernel,
        grid=(s // tm,),
        in_specs=[row(wcw), row(wrw), resident(wcw), resident(wrw), row(d),
                  vec, vec, vec, vec, vec],
        out_specs=out_specs,
        out_shape=out_shape,
        compiler_params=pltpu.CompilerParams(
            dimension_semantics=("arbitrary",),
            vmem_limit_bytes=56 * 1024 * 1024),
        name="outproj",
    )(yc, yr, wc, wr, x, gate, post, pre, scale, shift)
    return res if emit_h else (res[0], None)


def kernel(x, c, positions, norm_pre, norm_post, w_ada, b_ada, w_in, conv_w, gn_gain, w_out):
    batch, s, d = x.shape
    depth = w_in.shape[0]
    w_ret = gn_gain.shape[1]
    dv = w_ret // N_RET_HEADS
    dk = dv // 2
    half = dk // 2

    inv_freq = ROPE_THETA ** (-jnp.arange(half, dtype=F32) / half)
    freq_full = jnp.concatenate([inv_freq, inv_freq])[None, :]
    sign_full = jnp.concatenate([-jnp.ones((half,), F32), jnp.ones((half,), F32)])[None, :]
    log_gamma = jnp.log(1.0 - 2.0 ** (-5.0 - jnp.arange(N_RET_HEADS, dtype=F32)))
    lg_tab = jnp.broadcast_to(log_gamma[:, None, None], (N_RET_HEADS, SUBLANES, RET_CHUNK))

    outs = []
    for b in range(batch):
        xb = x[b]
        mod = _ada_call(c[b][:, None], w_ada, b_ada[:, None, :], tn=768)
        shift = [mod[l, :, 0:d] for l in range(depth)]
        scale = [mod[l, :, d:2 * d] for l in range(depth)]
        gate = [mod[l, :, 2 * d:3 * d] for l in range(depth)]
        cos_t, sin_t = _rope_call(positions[b][:, None], freq_full, sign_full, ts=1024)
        h = _prenorm_call(xb, norm_pre[0][None, :], scale[0], shift[0], tm=512)
        for l in range(depth):
            yc, yr, wc, wr = _mixer_call(h, w_in, l, conv_w[l], gn_gain[l][None, :], lg_tab,
                                         cos_t, sin_t, w_out, tm=512)
            last = l == depth - 1
            nl = l if last else l + 1
            xb, h = _outproj_call(yc, yr, wc, wr, xb, gate[l], norm_post[l][None, :],
                                  norm_pre[nl][None, :], scale[nl], shift[nl],
                                  tm=512, emit_h=not last)
        outs.append(xb)
    return outs[0][None] if batch == 1 else jnp.stack(outs, axis=0)
```

```python
import functools

import jax
import jax.numpy as jnp
from jax import lax
from jax.experimental import pallas as pl
from jax.experimental.pallas import tpu as pltpu

CONV_WIDTH = 3
N_RET_HEADS = 8
RET_CHUNK = 128
ROPE_THETA = 10000.0
NORM_EPS = 1e-6

V7X_VMEM_BYTES = 64 * 1024 * 1024
SUBLANES = 8
LANES = 128

F32 = jnp.float32
BF16 = jnp.bfloat16


def _silu(x):
    return x * (1.0 / (1.0 + jnp.exp(-x)))


def _rms_scale(x, d):
    return lax.rsqrt(jnp.sum(x * x, axis=-1, keepdims=True) * (1.0 / d) + NORM_EPS)


def _ada_kernel(c_ref, w_ref, b_ref, o_ref):
    s = _silu(c_ref[...])
    acc = jnp.sum(w_ref[0] * s, axis=0, keepdims=True)
    o_ref[0] = acc + b_ref[0]


def _ada_call(c_col, w_ada, b_ada3, tn):
    depth, d, n = w_ada.shape
    return pl.pallas_call(
        _ada_kernel,
        grid=(depth, n // tn),
        in_specs=[
            pl.BlockSpec((d, 1), lambda l, t: (0, 0)),
            pl.BlockSpec((1, d, tn), lambda l, t: (l, 0, t)),
            pl.BlockSpec((1, 1, tn), lambda l, t: (l, 0, t)),
        ],
        out_specs=pl.BlockSpec((1, 1, tn), lambda l, t: (l, 0, t)),
        out_shape=jax.ShapeDtypeStruct((depth, 1, n), F32),
        compiler_params=pltpu.CompilerParams(
            dimension_semantics=("arbitrary", "arbitrary"),
            vmem_limit_bytes=40 * 1024 * 1024),
        name="ada_mod",
    )(c_col, w_ada, b_ada3)


def _rope_kernel(pos_ref, freq_ref, sign_ref, cos_ref, sin_ref):
    ang = pos_ref[...].astype(F32) * freq_ref[...]
    cos_ref[...] = jnp.cos(ang)
    sin_ref[...] = jnp.sin(ang) * sign_ref[...]


def _rope_call(pos_col, freq_full, sign_full, ts):
    s = pos_col.shape[0]
    dk = freq_full.shape[1]
    return pl.pallas_call(
        _rope_kernel,
        grid=(s // ts,),
        in_specs=[
            pl.BlockSpec((ts, 1), lambda i: (i, 0)),
            pl.BlockSpec((1, dk), lambda i: (0, 0)),
            pl.BlockSpec((1, dk), lambda i: (0, 0)),
        ],
        out_specs=[pl.BlockSpec((ts, dk), lambda i: (i, 0)),
                   pl.BlockSpec((ts, dk), lambda i: (i, 0))],
        out_shape=[jax.ShapeDtypeStruct((s, dk), F32),
                   jax.ShapeDtypeStruct((s, dk), F32)],
        compiler_params=pltpu.CompilerParams(dimension_semantics=("arbitrary",)),
        name="rope_tables",
    )(pos_col, freq_full, sign_full)


def _prenorm_kernel(x_ref, gain_ref, scale_ref, shift_ref, h_ref):
    x = x_ref[...]
    d = x.shape[-1]
    y = x * _rms_scale(x, d) * gain_ref[...]
    h_ref[...] = (y * (1.0 + scale_ref[...]) + shift_ref[...]).astype(h_ref.dtype)


def _prenorm_call(x, gain, scale, shift, tm):
    s, d = x.shape
    vec = pl.BlockSpec((1, d), lambda i: (0, 0))
    return pl.pallas_call(
        _prenorm_kernel,
        grid=(s // tm,),
        in_specs=[pl.BlockSpec((tm, d), lambda i: (i, 0)), vec, vec, vec],
        out_specs=pl.BlockSpec((tm, d), lambda i: (i, 0)),
        out_shape=jax.ShapeDtypeStruct((s, d), BF16),
        compiler_params=pltpu.CompilerParams(dimension_semantics=("arbitrary",)),
        name="prenorm",
    )(x, gain, scale, shift)


def _mixer_kernel(h_ref, wq_ref, wk_ref, wv_ref, wr_ref, wb_ref, wc_ref, wu_ref, wg_ref,
                  convw_ref, gain_ref, lg_ref, cos_ref, sin_ref, woc_ref, wor_ref,
                  yc_ref, yr_ref, woc_out, wor_out,
                  w_sc, state_sc, carry_sc, *, gw, dk, dv, tm, rk, n_groups):
    p = pl.program_id(0)
    i = pl.program_id(1)
    o_q, o_k = 0, dk
    o_v = 2 * dk
    o_r = o_v + dv
    o_b = o_r + dv
    o_c, o_u, o_g = o_b + gw, o_b + 2 * gw, o_b + 3 * gw

    @pl.when(p < n_groups)
    def _():
        slot = p % 2
        rows = pl.ds(pl.multiple_of(i * rk, rk), rk)
        for ref, off in ((wq_ref, o_q), (wk_ref, o_k), (wv_ref, o_v), (wr_ref, o_r),
                         (wb_ref, o_b), (wc_ref, o_c), (wu_ref, o_u), (wg_ref, o_g)):
            w_sc[slot, rows, off:off + ref.shape[1]] = ref[...].astype(BF16)

    woc_out[...] = woc_ref[...].astype(BF16)
    wor_out[...] = wor_ref[...].astype(BF16)

    @pl.when(p > 0)
    def _():
        _mixer_compute(h_ref, convw_ref, gain_ref, lg_ref, cos_ref, sin_ref, yc_ref, yr_ref,
                       w_sc.at[(p + 1) % 2], state_sc, carry_sc,
                       (o_q, o_v, o_r, o_b, o_c, o_u, o_g), gw=gw, dk=dk, dv=dv, tm=tm)


def _mixer_compute(h_ref, convw_ref, gain_ref, lg_ref, cos_ref, sin_ref, yc_ref, yr_ref,
                   w_ref, state_sc, carry_sc, offs, *, gw, dk, dv, tm):
    o_q, o_v, o_r, o_b, o_c, o_u, o_g = offs

    @pl.when(pl.program_id(1) == 0)
    def _():
        state_sc[...] = jnp.zeros_like(state_sc)
        carry_sc[...] = jnp.zeros_like(carry_sc)

    h = h_ref[...]

    def proj(off, width):
        return jnp.dot(h, w_ref[:, off:off + width], preferred_element_type=F32)

    cosv = cos_ref[...]
    sinv = sin_ref[...]
    qk = proj(o_q, 2 * dk)
    q = qk[:, 0:dk]
    k = qk[:, dk:2 * dk]
    qr = q * cosv + pltpu.roll(q, dk // 2, 1) * sinv
    kr = (k * cosv + pltpu.roll(k, dk // 2, 1) * sinv) * (dk ** -0.5)
    v = proj(o_v, dv)
    gate_r = _silu(proj(o_r, dv))

    lg = lg_ref[0][0:1, :]
    ch = RET_CHUNK
    ri = lax.broadcasted_iota(jnp.int32, (ch, ch), 0)
    ci = lax.broadcasted_iota(jnp.int32, (ch, ch), 1)
    diff = (ri - ci).astype(F32)
    decay = jnp.where(diff >= 0.0, jnp.exp(jnp.maximum(diff, 0.0) * lg), 0.0)
    rif = ri.astype(F32)
    zeta = jnp.exp((ch - 1.0 - rif) * lg)
    xi = jnp.exp((rif + 1.0) * lg)
    xi_v = jnp.concatenate([xi] * (dv // ch), axis=1) if dv != ch else xi
    gamma_chunk = jnp.exp(ch * lg)
    gamma_v = jnp.concatenate([gamma_chunk] * (dv // ch), axis=1) if dv != ch else gamma_chunk
    gain = gain_ref[...]

    state = state_sc[...]
    for c in range(tm // ch):
        r0 = c * ch
        qc = qr[r0:r0 + ch].astype(BF16)
        kc = kr[r0:r0 + ch]
        vc = v[r0:r0 + ch].astype(BF16)
        scores = lax.dot_general(qc, kc.astype(BF16), (((1,), (1,)), ((), ())),
                                 preferred_element_type=F32) * decay
        inner = jnp.dot(scores.astype(BF16), vc, preferred_element_type=F32)
        cross = jnp.dot(qc, state.astype(BF16), preferred_element_type=F32) * xi_v
        kz = (kc * zeta).astype(BF16)
        kv = lax.dot_general(kz, vc, (((0,), (0,)), ((), ())), preferred_element_type=F32)
        state = gamma_v * state + kv
        y = inner + cross
        mu = jnp.sum(y, axis=-1, keepdims=True) * (1.0 / dv)
        yc = y - mu
        var = jnp.sum(yc * yc, axis=-1, keepdims=True) * (1.0 / dv)
        yn = yc * lax.rsqrt(var + NORM_EPS) * gain
        yr_ref[r0:r0 + ch, :] = (yn * gate_r[r0:r0 + ch]).astype(yr_ref.dtype)
    state_sc[...] = state

    cu = proj(o_c, gw) * proj(o_u, gw)
    zb = proj(o_b, gw)
    gate_c = _silu(proj(o_g, gw))
    w0 = convw_ref[0:1, :]
    w1 = convw_ref[1:2, :]
    w2 = convw_ref[2:3, :]
    conv = w0 * pltpu.roll(cu, 2, 0) + w1 * pltpu.roll(cu, 1, 0) + w2 * cu
    yc_ref[...] = (zb * conv * gate_c).astype(yc_ref.dtype)
    head = cu[0:SUBLANES]
    prev = carry_sc[...]
    row = lax.broadcasted_iota(jnp.int32, head.shape, 0)
    m1 = jnp.where(row >= 1, pltpu.roll(head, 1, 0), pltpu.roll(prev, 1, 0))
    m2 = jnp.where(row >= 2, pltpu.roll(head, 2, 0), pltpu.roll(prev, 2, 0))
    conv_head = w0 * m2 + w1 * m1 + w2 * head
    yc_ref[0:SUBLANES, :] = (zb[0:SUBLANES] * conv_head * gate_c[0:SUBLANES]).astype(yc_ref.dtype)
    carry_sc[...] = cu[tm - SUBLANES:tm]


def _mixer_call(h, w_in, layer, conv_w, gn_gain2, lg_tab, cos_t, sin_t, w_out, tm):
    s, d = h.shape
    w_conv = conv_w.shape[1]
    n_groups = N_RET_HEADS
    gw = w_conv // n_groups
    w_ret = gn_gain2.shape[1]
    dv = w_ret // n_groups
    dk = dv // 2
    w_qk = n_groups * dk
    n_tiles = s // tm
    rk = d // n_tiles
    ro_c = gw // n_tiles
    ro_r = dv // n_tiles
    b_c, b_u, b_g = w_conv // gw, 2 * w_conv // gw, 3 * w_conv // gw
    b_q = 4 * w_conv // dk
    b_k = b_q + w_qk // dk
    b_v = (4 * w_conv + 2 * w_qk) // dv
    b_r = b_v + w_ret // dv
    n_cols = 4 * gw + 2 * dk + 2 * dv

    def group(p):
        return jnp.maximum(p - 1, 0)

    def tile(p, i):
        return jnp.where(p == 0, 0, i)

    def chunk(p, i):
        return jnp.where(p < n_groups, i, n_tiles - 1)

    def loaded(p):
        return jnp.minimum(p, n_groups - 1)

    def wspec(width, base):
        return pl.BlockSpec((None, rk, width),
                            lambda p, i: (layer, chunk(p, i), base + loaded(p)))

    def wo_row(p, i):
        return loaded(p) * n_tiles + chunk(p, i)

    kern = functools.partial(_mixer_kernel, gw=gw, dk=dk, dv=dv, tm=tm, rk=rk, n_groups=n_groups)
    return pl.pallas_call(
        kern,
        grid=(n_groups + 1, n_tiles),
        in_specs=[
            pl.BlockSpec((tm, d), lambda p, i: (tile(p, i), 0)),
            wspec(dk, b_q), wspec(dk, b_k), wspec(dv, b_v), wspec(dv, b_r),
            wspec(gw, 0), wspec(gw, b_c), wspec(gw, b_u), wspec(gw, b_g),
            pl.BlockSpec((CONV_WIDTH, gw), lambda p, i: (0, group(p))),
            pl.BlockSpec((1, dv), lambda p, i: (0, group(p))),
            pl.BlockSpec((1, SUBLANES, RET_CHUNK), lambda p, i: (group(p), 0, 0)),
            pl.BlockSpec((tm, dk), lambda p, i: (tile(p, i), 0)),
            pl.BlockSpec((tm, dk), lambda p, i: (tile(p, i), 0)),
            pl.BlockSpec((None, ro_c, d), lambda p, i: (layer, wo_row(p, i), 0)),
            pl.BlockSpec((None, ro_r, d),
                         lambda p, i: (layer, w_conv // ro_r + wo_row(p, i), 0)),
        ],
        out_specs=[pl.BlockSpec((tm, gw), lambda p, i: (tile(p, i), group(p))),
                   pl.BlockSpec((tm, dv), lambda p, i: (tile(p, i), group(p))),
                   pl.BlockSpec((ro_c, d), lambda p, i: (wo_row(p, i), 0)),
                   pl.BlockSpec((ro_r, d), lambda p, i: (wo_row(p, i), 0))],
        out_shape=[jax.ShapeDtypeStruct((s, w_conv), BF16),
                   jax.ShapeDtypeStruct((s, w_ret), BF16),
                   jax.ShapeDtypeStruct((w_conv, d), BF16),
                   jax.ShapeDtypeStruct((w_ret, d), BF16)],
        scratch_shapes=[pltpu.VMEM((2, d, n_cols), BF16),
                        pltpu.VMEM((dk, dv), F32),
                        pltpu.VMEM((SUBLANES, gw), F32)],
        compiler_params=pltpu.CompilerParams(
            dimension_semantics=("arbitrary", "arbitrary"),
            vmem_limit_bytes=56 * 1024 * 1024),
        name="mixer",
    )(h, w_in, w_in, w_in, w_in, w_in, w_in, w_in, w_in,
      conv_w, gn_gain2, lg_tab, cos_t, sin_t, w_out, w_out)


def _outproj_kernel(yc_ref, yr_ref, wc_ref, wr_ref, x_ref, gate_ref, post_ref,
                    pre_ref, scale_ref, shift_ref, xo_ref, *maybe_h_ref):
    o = jnp.dot(yc_ref[...], wc_ref[...], preferred_element_type=F32)
    o = o + jnp.dot(yr_ref[...], wr_ref[...], preferred_element_type=F32)
    d = o.shape[-1]
    r = o * _rms_scale(o, d) * post_ref[...]
    xn = x_ref[...] + gate_ref[...] * r
    xo_ref[...] = xn
    if maybe_h_ref:
        h_ref, = maybe_h_ref
        y = xn * _rms_scale(xn, d) * pre_ref[...]
        h_ref[...] = (y * (1.0 + scale_ref[...]) + shift_ref[...]).astype(h_ref.dtype)


def _outproj_call(yc, yr, wc, wr, x, gate, post, pre, scale, shift, tm, emit_h):
    s, d = x.shape
    wcw = yc.shape[1]
    wrw = yr.shape[1]
    vec = pl.BlockSpec((1, d), lambda i: (0, 0))
    row = lambda w: pl.BlockSpec((tm, w), lambda i: (i, 0))
    resident = lambda w: pl.BlockSpec((w, d), lambda i: (0, 0), pipeline_mode=pl.Buffered(1))
    out_specs = [row(d)]
    out_shape = [jax.ShapeDtypeStruct((s, d), F32)]
    if emit_h:
        out_specs.append(row(d))
        out_shape.append(jax.ShapeDtypeStruct((s, d), BF16))
    res = pl.pallas_call(
        _outproj_kernel,
        grid=(s // tm,),
        in_specs=[row(wcw), row(wrw), resident(wcw), resident(wrw), row(d),
                  vec, vec, vec, vec, vec],
        out_specs=out_specs,
        out_shape=out_shape,
        compiler_params=pltpu.CompilerParams(
            dimension_semantics=("arbitrary",),
            vmem_limit_bytes=56 * 1024 * 1024),
        name="outproj",
    )(yc, yr, wc, wr, x, gate, post, pre, scale, shift)
    return res if emit_h else (res[0], None)


def kernel(x, c, positions, norm_pre, norm_post, w_ada, b_ada, w_in, conv_w, gn_gain, w_out):
    batch, s, d = x.shape
    depth = w_in.shape[0]
    w_ret = gn_gain.shape[1]
    dv = w_ret // N_RET_HEADS
    dk = dv // 2
    half = dk // 2

    inv_freq = ROPE_THETA ** (-jnp.arange(half, dtype=F32) / half)
    freq_full = jnp.concatenate([inv_freq, inv_freq])[None, :]
    sign_full = jnp.concatenate([-jnp.ones((half,), F32), jnp.ones((half,), F32)])[None, :]
    log_gamma = jnp.log(1.0 - 2.0 ** (-5.0 - jnp.arange(N_RET_HEADS, dtype=F32)))
    lg_tab = jnp.broadcast_to(log_gamma[:, None, None], (N_RET_HEADS, SUBLANES, RET_CHUNK))

    outs = []
    for b in range(batch):
        xb = x[b]
        mod = _ada_call(c[b][:, None], w_ada, b_ada[:, None, :], tn=768)
        shift = [mod[l, :, 0:d] for l in range(depth)]
        scale = [mod[l, :, d:2 * d] for l in range(depth)]
        gate = [mod[l, :, 2 * d:3 * d] for l in range(depth)]
        cos_t, sin_t = _rope_call(positions[b][:, None], freq_full, sign_full, ts=1024)
        h = _prenorm_call(xb, norm_pre[0][None, :], scale[0], shift[0], tm=512)
        for l in range(depth):
            yc, yr, wc, wr = _mixer_call(h, w_in, l, conv_w[l], gn_gain[l][None, :], lg_tab,
                                         cos_t, sin_t, w_out, tm=1024)
            last = l == depth - 1
            nl = l if last else l + 1
            xb, h = _outproj_call(yc, yr, wc, wr, xb, gate[l], norm_post[l][None, :],
                                  norm_pre[nl][None, :], scale[nl], shift[nl],
                                  tm=512, emit_h=not last)
        outs.append(xb)
    return outs[0][None] if batch == 1 else jnp.stack(outs, axis=0)
```
